```python
import math
import jax, jax.numpy as jnp
from jax import lax
import numpy as np

D_MODEL = 1024
BATCH = 2
SEQ = 8192
DEPTH = 2
DEC_BATCH = 128
DEC_SEQ = 1
PAST_LEN = 16384
PAGE_SIZE = 128

GM_HEADS = 4
GM_HEAD_DIM = 64
GM_WIDTH = GM_HEADS * GM_HEAD_DIM
CHUNK = 128
SSM_GROUP = 16
SSM_WIDTH = 256
SSM_GROUPS = SSM_WIDTH // SSM_GROUP
SSM_STATE = 64
MLA_HEADS = 8
QK_NOPE = 64
QK_ROPE = 32
V_HEAD = 64
Q_LORA = 256
KV_LORA = 128
MLA_WIDTH = MLA_HEADS * V_HEAD
ROPE_THETA = 10000.0
MLA_SCALE = (QK_NOPE + QK_ROPE) ** -0.5
Q_BLOCK = 128
D_MIX = GM_WIDTH + SSM_WIDTH + MLA_WIDTH
IN_SPLITS = [GM_WIDTH, 2 * GM_WIDTH, 2 * GM_WIDTH + SSM_WIDTH,
             2 * GM_WIDTH + SSM_WIDTH + Q_LORA, 2 * GM_WIDTH + SSM_WIDTH + Q_LORA + KV_LORA]
IN_COLS = 2 * GM_WIDTH + SSM_WIDTH + Q_LORA + KV_LORA + QK_ROPE
N_MEM = 256
MEM_HEADS = 4
MEM_HEAD_DIM = D_MODEL // MEM_HEADS
MEM_SCALE = MEM_HEAD_DIM ** -0.5
D_FF = -(-8 * D_MODEL // (3 * 256)) * 256
EPS = 1e-6

kernel_name = "hymba_gmlp_s5_mla_step"

F32 = jnp.float32
NEG = float(np.finfo(np.float32).min)


def rmsnorm(x, g):
    xf = x.astype(F32)
    y = xf * lax.rsqrt(jnp.mean(xf * xf, -1, keepdims=True) + EPS)
    return (y * g.astype(F32)).astype(x.dtype)


def layernorm(x, g, b):
    xf = x.astype(F32)
    mu = jnp.mean(xf, -1, keepdims=True)
    var = jnp.mean(jnp.square(xf - mu), -1, keepdims=True)
    return ((xf - mu) * lax.rsqrt(var + EPS) * g.astype(F32) + b.astype(F32)).astype(x.dtype)


def rope(x, pos):
    half = QK_ROPE // 2
    inv = ROPE_THETA ** (-jnp.arange(half, dtype=F32) / half)
    ang = pos.astype(F32)[:, None] * inv[None, :]
    cos = jnp.cos(ang)[None, :, None, :]
    sin = jnp.sin(ang)[None, :, None, :]
    xf = x.astype(F32)
    x1, x2 = xf[..., :half], xf[..., half:]
    return jnp.concatenate([x1 * cos - x2 * sin, x1 * sin + x2 * cos], -1).astype(x.dtype)


def chunk_mlp(u, v, ln_g, ln_b, w_sp, b_sp):
    bsz, L, _ = v.shape
    vn = layernorm(v, ln_g, ln_b)
    pad = (-L) % CHUNK
    nc = (L + pad) // CHUNK
    vc = jnp.pad(vn, ((0, 0), (0, pad), (0, 0))).reshape(bsz, nc, CHUNK, GM_HEADS, GM_HEAD_DIM)
    causal = jnp.tril(jnp.ones((CHUNK, CHUNK), bool))
    w = jnp.where(causal[None], w_sp, 0)
    mixed = jnp.einsum('hts,bcshd->bcthd', w, vc) + b_sp.T[None, None, :, :, None]
    mixed = mixed.reshape(bsz, nc * CHUNK, GM_WIDTH)[:, :L]
    return u * mixed, vn


def ssm_discretise(a_re, a_im, log_dt, b_re, b_im):
    a_re = a_re.astype(F32); a_im = a_im.astype(F32)
    dt = jnp.exp(log_dt.astype(F32))[:, None]
    mag = jnp.exp(a_re * dt)
    ab_re = mag * jnp.cos(a_im * dt)
    ab_im = mag * jnp.sin(a_im * dt)
    er, ei = ab_re - 1.0, ab_im
    den = a_re * a_re + a_im * a_im
    fr = ((er * a_re + ei * a_im) / den)[..., None]
    fi = ((ei * a_re - er * a_im) / den)[..., None]
    b_re = b_re.astype(F32); b_im = b_im.astype(F32)
    return ab_re, ab_im, fr * b_re - fi * b_im, fr * b_im + fi * b_re


def _complex_affine_combine(e1, e2):
    a1r, a1i, b1r, b1i = e1
    a2r, a2i, b2r, b2i = e2
    return (a2r * a1r - a2i * a1i, a2r * a1i + a2i * a1r,
            a2r * b1r - a2i * b1i + b2r, a2r * b1i + a2i * b1r + b2i)


def s5_mixer(s, h0_re, h0_im, a_re, a_im, log_dt, b_re, b_im, c_re, c_im, d_skip, w_glu, b_glu):
    bsz, L, _ = s.shape
    u = s.astype(F32).reshape(bsz, L, SSM_GROUPS, SSM_GROUP)
    ab_re, ab_im, bb_re, bb_im = ssm_discretise(a_re, a_im, log_dt, b_re, b_im)
    bu_re = jnp.einsum('blgh,gph->blgp', u, bb_re)
    bu_im = jnp.einsum('blgh,gph->blgp', u, bb_im)
    h0_re = h0_re.astype(F32); h0_im = h0_im.astype(F32)
    bu_re = bu_re.at[:, 0].add(ab_re * h0_re - ab_im * h0_im)
    bu_im = bu_im.at[:, 0].add(ab_re * h0_im + ab_im * h0_re)
    aa_re = jnp.broadcast_to(ab_re, bu_re.shape)
    aa_im = jnp.broadcast_to(ab_im, bu_im.shape)
    _, _, h_re, h_im = lax.associative_scan(_complex_affine_combine, (aa_re, aa_im, bu_re, bu_im), axis=1)
    y = (jnp.einsum('blgp,ghp->blgh', h_re, c_re.astype(F32))
         - jnp.einsum('blgp,ghp->blgh', h_im, c_im.astype(F32)))
    y = y.reshape(bsz, L, SSM_WIDTH) + d_skip.astype(F32) * s.astype(F32)
    y = jax.nn.gelu(y)
    out = y * jax.nn.sigmoid(y @ w_glu.astype(F32) + b_glu.astype(F32))
    return out.astype(s.dtype), h_re[:, -1], h_im[:, -1]


def mla_project(z_q, z_kv, z_kr, pos, q_norm_g, kv_norm_g, w_uq):
    bsz, L, _ = z_q.shape
    q = (rmsnorm(z_q, q_norm_g) @ w_uq).reshape(bsz, L, MLA_HEADS, QK_NOPE + QK_ROPE)
    q_nope, q_rope = q[..., :QK_NOPE], rope(q[..., QK_NOPE:], pos)
    c = rmsnorm(z_kv, kv_norm_g)
    k_rope = rope(z_kr[:, :, None, :], pos)[:, :, 0]
    return q_nope, q_rope, c, k_rope


def mla_prompt_attn(q_nope, q_rope, c, k_rope, w_uk, w_uv):
    bsz, L = c.shape[0], c.shape[1]
    k_nope = jnp.einsum('blc,chd->blhd', c, w_uk)
    v = jnp.einsum('blc,chd->blhd', c, w_uv)
    q = jnp.concatenate([q_nope, q_rope], -1)
    k = jnp.concatenate([k_nope, jnp.broadcast_to(k_rope[:, :, None, :], (bsz, L, MLA_HEADS, QK_ROPE))], -1)
    kpos = jnp.arange(L)

    def block(i):
        qb = lax.dynamic_slice_in_dim(q, i * Q_BLOCK, Q_BLOCK, axis=1)
        sc = jnp.einsum('bqhd,bkhd->bhqk', qb, k).astype(F32) * MLA_SCALE
        qpos = i * Q_BLOCK + jnp.arange(Q_BLOCK)
        sc = jnp.where(kpos[None, :] <= qpos[:, None], sc, NEG)
        p = jax.nn.softmax(sc, axis=-1).astype(v.dtype)
        return jnp.einsum('bhqk,bkhd->bqhd', p, v)

    out = lax.map(block, jnp.arange(L // Q_BLOCK))
    return out.transpose(1, 0, 2, 3, 4).reshape(bsz, L, MLA_WIDTH)


def mla_paged_attn(q_nope, q_rope, c_new, kr_new, w_uk, w_uv, c_past, kr_past):
    bsz, S = c_new.shape[0], c_new.shape[1]
    q_lat = jnp.einsum('bshd,chd->bshc', q_nope, w_uk)
    s_past = (jnp.einsum('bshc,btc->bhst', q_lat, c_past)
              + jnp.einsum('bshr,btr->bhst', q_rope, kr_past)).astype(F32)
    s_new = (jnp.einsum('bshc,btc->bhst', q_lat, c_new)
             + jnp.einsum('bshr,btr->bhst', q_rope, kr_new)).astype(F32)
    causal = jnp.tril(jnp.ones((S, S), bool))
    s_new = jnp.where(causal, s_new, NEG)
    p = jax.nn.softmax(jnp.concatenate([s_past, s_new], -1) * MLA_SCALE, axis=-1).astype(c_new.dtype)
    n_past = c_past.shape[1]
    o_lat = (jnp.einsum('bhst,btc->bshc', p[..., :n_past], c_past)
             + jnp.einsum('bhst,btc->bshc', p[..., n_past:], c_new))
    return jnp.einsum('bshc,chd->bshd', o_lat, w_uv).reshape(bsz, S, MLA_WIDTH)


def mem_kv(mem, mem_norm_g, w_mk, w_mv):
    bsz, m = mem.shape[0], mem.shape[1]
    mn = rmsnorm(mem, mem_norm_g)
    k = (mn @ w_mk).reshape(bsz, m, MEM_HEADS, MEM_HEAD_DIM)
    v = (mn @ w_mv).reshape(bsz, m, MEM_HEADS, MEM_HEAD_DIM)
    return k, v


def mem_attend(h, k, v, w_mq, w_mo):
    bsz, L, _ = h.shape
    q = (h @ w_mq).reshape(bsz, L, MEM_HEADS, MEM_HEAD_DIM)
    sc = jnp.einsum('blhe,bmhe->bhlm', q, k).astype(F32) * MEM_SCALE
    p = jax.nn.softmax(sc, axis=-1).astype(v.dtype)
    o = jnp.einsum('bhlm,bmhe->blhe', p, v).reshape(bsz, L, D_MODEL)
    return o @ w_mo


def swiglu(h, w_gate, w_up, w_down):
    return (jax.nn.silu(h @ w_gate) * (h @ w_up)) @ w_down


def hybrid_layer(x, pos, mem_k, mem_v, h0_re, h0_im, attend, lp):
    h = rmsnorm(x, lp['g_pre_mix'])
    z = h @ lp['w_in']
    u_a, v_a, s_b, z_q, z_kv, z_kr = jnp.split(z, IN_SPLITS, axis=-1)
    a_out, v_rows = chunk_mlp(jax.nn.gelu(u_a), jax.nn.gelu(v_a), lp['gm_ln_g'], lp['gm_ln_b'],
                              lp['gm_w_sp'], lp['gm_b_sp'])
    b_out, h_re, h_im = s5_mixer(s_b, h0_re, h0_im, lp['ssm_a_re'], lp['ssm_a_im'], lp['ssm_log_dt'],
                                 lp['ssm_b_re'], lp['ssm_b_im'], lp['ssm_c_re'], lp['ssm_c_im'],
                                 lp['ssm_d'], lp['ssm_w_glu'], lp['ssm_b_glu'])
    q_nope, q_rope, c, k_rope = mla_project(z_q, z_kv, z_kr, pos, lp['mla_q_norm'], lp['mla_kv_norm'],
                                            lp['mla_w_uq'])
    c_out = attend(q_nope, q_rope, c, k_rope, lp['mla_w_uk'], lp['mla_w_uv'])
    mix = jnp.concatenate([a_out, b_out, c_out], -1) @ lp['w_out']
    x = x + rmsnorm(mix, lp['g_post_mix'])
    m = mem_attend(rmsnorm(x, lp['g_pre_mem']), mem_k, mem_v, lp['mem_w_q'], lp['mem_w_o'])
    x = x + rmsnorm(m, lp['g_post_mem'])
    f = swiglu(rmsnorm(x, lp['g_pre_ffn']), lp['ffn_w_gate'], lp['ffn_w_up'], lp['ffn_w_down'])
    x = x + rmsnorm(f, lp['g_post_ffn'])
    return x, c, k_rope, h_re, h_im, v_rows


def setup_inputs(seed: int = 0) -> dict:
    key = jax.random.key(seed)
    keys = iter(jax.random.split(key, 64))

    def normal(shape, scale=1.0):
        return scale * jax.random.normal(next(keys), shape, F32)

    def gain(shape):
        return 1.0 + normal(shape, 0.02)

    n_pages = PAST_LEN // PAGE_SIZE
    used = DEC_BATCH * n_pages
    n_pool = used + max(1, used // 4)
    page_table = jax.random.permutation(next(keys), n_pool)[:used].reshape(DEC_BATCH, n_pages).astype(jnp.int32)

    log_dt = jax.random.uniform(next(keys), (DEPTH, SSM_GROUPS), F32, math.log(1e-3), math.log(1e-1))
    a_re = -0.5 + normal((DEPTH, SSM_GROUPS, SSM_STATE), 0.01)
    a_im = math.pi * jnp.arange(SSM_STATE, dtype=F32)[None, None, :] + normal((DEPTH, SSM_GROUPS, SSM_STATE), 0.01)

    return {
        'x_prompt': normal((BATCH, SEQ, D_MODEL)),
        'x_sample': normal((DEC_BATCH, DEC_SEQ, D_MODEL)),
        'cache_kv_latent': normal((DEPTH, n_pool, PAGE_SIZE, KV_LORA)),
        'cache_k_rope': normal((DEPTH, n_pool, PAGE_SIZE, QK_ROPE)),
        'cache_mem_k': normal((DEPTH, DEC_BATCH, N_MEM, MEM_HEADS, MEM_HEAD_DIM)),
        'cache_mem_v': normal((DEPTH, DEC_BATCH, N_MEM, MEM_HEADS, MEM_HEAD_DIM)),
        'state_ssm_re': normal((DEPTH, DEC_BATCH, SSM_GROUPS, SSM_STATE), 0.3),
        'state_ssm_im': normal((DEPTH, DEC_BATCH, SSM_GROUPS, SSM_STATE), 0.3),
        'page_table': page_table,
        'mem_prompt': normal((BATCH, N_MEM, D_MODEL)),
        'g_pre_mix': gain((DEPTH, D_MODEL)),
        'g_post_mix': gain((DEPTH, D_MODEL)),
        'g_pre_mem': gain((DEPTH, D_MODEL)),
        'g_post_mem': gain((DEPTH, D_MODEL)),
        'g_pre_ffn': gain((DEPTH, D_MODEL)),
        'g_post_ffn': gain((DEPTH, D_MODEL)),
        'w_in': normal((DEPTH, D_MODEL, IN_COLS), D_MODEL ** -0.5),
        'w_out': normal((DEPTH, D_MIX, D_MODEL), D_MIX ** -0.5),
        'gm_ln_g': gain((DEPTH, GM_WIDTH)),
        'gm_ln_b': normal((DEPTH, GM_WIDTH), 0.01),
        'gm_w_sp': normal((DEPTH, GM_HEADS, CHUNK, CHUNK), CHUNK ** -0.5),
        'gm_b_sp': 1.0 + normal((DEPTH, GM_HEADS, CHUNK), 0.1),
        'ssm_a_re': a_re,
        'ssm_a_im': a_im,
        'ssm_log_dt': log_dt,
        'ssm_b_re': normal((DEPTH, SSM_GROUPS, SSM_STATE, SSM_GROUP), (2 * SSM_GROUP) ** -0.5),
        'ssm_b_im': normal((DEPTH, SSM_GROUPS, SSM_STATE, SSM_GROUP), (2 * SSM_GROUP) ** -0.5),
        'ssm_c_re': normal((DEPTH, SSM_GROUPS, SSM_GROUP, SSM_STATE), (2 * SSM_STATE) ** -0.5),
        'ssm_c_im': normal((DEPTH, SSM_GROUPS, SSM_GROUP, SSM_STATE), (2 * SSM_STATE) ** -0.5),
        'ssm_d': normal((DEPTH, SSM_WIDTH)),
        'ssm_w_glu': normal((DEPTH, SSM_WIDTH, SSM_WIDTH), SSM_WIDTH ** -0.5),
        'ssm_b_glu': normal((DEPTH, SSM_WIDTH), 0.01),
        'mla_q_norm': gain((DEPTH, Q_LORA)),
        'mla_kv_norm': gain((DEPTH, KV_LORA)),
        'mla_w_uq': normal((DEPTH, Q_LORA, MLA_HEADS * (QK_NOPE + QK_ROPE)), Q_LORA ** -0.5),
        'mla_w_uk': normal((DEPTH, KV_LORA, MLA_HEADS, QK_NOPE), KV_LORA ** -0.5),
        'mla_w_uv': normal((DEPTH, KV_LORA, MLA_HEADS, V_HEAD), KV_LORA ** -0.5),
        'mem_norm': gain((DEPTH, D_MODEL)),
        'mem_w_q': normal((DEPTH, D_MODEL, D_MODEL), D_MODEL ** -0.5),
        'mem_w_k': normal((DEPTH, D_MODEL, D_MODEL), D_MODEL ** -0.5),
        'mem_w_v': normal((DEPTH, D_MODEL, D_MODEL), D_MODEL ** -0.5),
        'mem_w_o': normal((DEPTH, D_MODEL, D_MODEL), D_MODEL ** -0.5),
        'ffn_w_gate': normal((DEPTH, D_MODEL, D_FF), D_MODEL ** -0.5),
        'ffn_w_up': normal((DEPTH, D_MODEL, D_FF), D_MODEL ** -0.5),
        'ffn_w_down': normal((DEPTH, D_FF, D_MODEL), D_FF ** -0.5),
    }


def reference(x_prompt, x_sample, cache_kv_latent, cache_k_rope, cache_mem_k, cache_mem_v,
              state_ssm_re, state_ssm_im, page_table, mem_prompt,
              g_pre_mix, g_post_mix, g_pre_mem, g_post_mem, g_pre_ffn, g_post_ffn,
              w_in, w_out, gm_ln_g, gm_ln_b, gm_w_sp, gm_b_sp,
              ssm_a_re, ssm_a_im, ssm_log_dt, ssm_b_re, ssm_b_im, ssm_c_re, ssm_c_im,
              ssm_d, ssm_w_glu, ssm_b_glu,
              mla_q_norm, mla_kv_norm, mla_w_uq, mla_w_uk, mla_w_uv,
              mem_norm, mem_w_q, mem_w_k, mem_w_v, mem_w_o,
              ffn_w_gate, ffn_w_up, ffn_w_down):
    bsz, seq = x_prompt.shape[0], x_prompt.shape[1]
    dbsz, dseq = x_sample.shape[0], x_sample.shape[1]
    pos_p = jnp.arange(seq)
    pos_s = PAST_LEN + jnp.arange(dseq)
    h0_p = jnp.zeros((bsz, SSM_GROUPS, SSM_STATE), F32)

    xp, xs = x_prompt, x_sample
    kvl_p, kr_p, mk_p, mv_p, sre_p, sim_p = [], [], [], [], [], []
    kvl_s, kr_s, sre_s, sim_s, gv_s = [], [], [], [], []
    for l in range(DEPTH):
        lp = dict(g_pre_mix=g_pre_mix[l], g_post_mix=g_post_mix[l], g_pre_mem=g_pre_mem[l],
                  g_post_mem=g_post_mem[l], g_pre_ffn=g_pre_ffn[l], g_post_ffn=g_post_ffn[l],
                  w_in=w_in[l], w_out=w_out[l], gm_ln_g=gm_ln_g[l], gm_ln_b=gm_ln_b[l],
                  gm_w_sp=gm_w_sp[l], gm_b_sp=gm_b_sp[l],
                  ssm_a_re=ssm_a_re[l], ssm_a_im=ssm_a_im[l], ssm_log_dt=ssm_log_dt[l],
                  ssm_b_re=ssm_b_re[l], ssm_b_im=ssm_b_im[l], ssm_c_re=ssm_c_re[l], ssm_c_im=ssm_c_im[l],
                  ssm_d=ssm_d[l], ssm_w_glu=ssm_w_glu[l], ssm_b_glu=ssm_b_glu[l],
                  mla_q_norm=mla_q_norm[l], mla_kv_norm=mla_kv_norm[l], mla_w_uq=mla_w_uq[l],
                  mla_w_uk=mla_w_uk[l], mla_w_uv=mla_w_uv[l],
                  mem_w_q=mem_w_q[l], mem_w_o=mem_w_o[l],
                  ffn_w_gate=ffn_w_gate[l], ffn_w_up=ffn_w_up[l], ffn_w_down=ffn_w_down[l])

        mk, mv = mem_kv(mem_prompt, mem_norm[l], mem_w_k[l], mem_w_v[l])
        xp, c, kr, hre, him, _ = hybrid_layer(xp, pos_p, mk, mv, h0_p, h0_p, mla_prompt_attn, lp)
        kvl_p.append(c); kr_p.append(kr); mk_p.append(mk); mv_p.append(mv)
        sre_p.append(hre); sim_p.append(him)

        c_past = cache_kv_latent[l, page_table].reshape(dbsz, -1, KV_LORA)
        kr_past = cache_k_rope[l, page_table].reshape(dbsz, -1, QK_ROPE)
        attend = lambda qn, qr, cn, krn, wuk, wuv, cp=c_past, kp=kr_past: mla_paged_attn(qn, qr, cn, krn, wuk, wuv, cp, kp)
        xs, c, kr, hre, him, vrows = hybrid_layer(xs, pos_s, cache_mem_k[l], cache_mem_v[l],
                                                  state_ssm_re[l], state_ssm_im[l], attend, lp)
        kvl_s.append(c); kr_s.append(kr); sre_s.append(hre); sim_s.append(him); gv_s.append(vrows)

    return (xp, xs,
            jnp.stack(kvl_p), jnp.stack(kr_p), jnp.stack(mk_p), jnp.stack(mv_p),
            jnp.stack(sre_p), jnp.stack(sim_p),
            jnp.stack(kvl_s), jnp.stack(kr_s), jnp.stack(sre_s), jnp.stack(sim_s), jnp.stack(gv_s))
```

```python
import functools
import math

import jax
import jax.numpy as jnp
from jax import lax
from jax.experimental import pallas as pl
from jax.experimental.pallas import tpu as pltpu

F32 = jnp.float32
BF16 = jnp.bfloat16

EPS = 1e-6
ROPE_THETA = 10000.0
PAGE_SIZE = 128
CHUNK = 128
GM_HEADS = 4
GM_HEAD_DIM = 64
GM_WIDTH = GM_HEADS * GM_HEAD_DIM
SSM_GROUP = 16
SSM_STATE = 64
SSM_WIDTH = 256
SSM_GROUPS = SSM_WIDTH // SSM_GROUP
SSM_STATES = SSM_GROUPS * SSM_STATE
MLA_HEADS = 8
QK_NOPE = 64
QK_ROPE = 32
V_HEAD = 64
Q_LORA = 256
KV_LORA = 128
MLA_SCALE = (QK_NOPE + QK_ROPE) ** -0.5
MEM_HEADS = 4
LANE = 128
QW = 2 * LANE
IN_PAD = 1280
NEG = float(jnp.finfo(jnp.float32).min)
VMEM_LIMIT = 56 * 1024 * 1024


def _bdot(a, b):
    return jnp.dot(a.astype(BF16), b.astype(BF16), preferred_element_type=F32)


def _bdot_nt(a, b):
    return lax.dot_general(a.astype(BF16), b.astype(BF16), (((1,), (1,)), ((), ())),
                           preferred_element_type=F32)


def _rms(x, g):
    return x * lax.rsqrt(jnp.mean(x * x, -1, keepdims=True) + EPS) * g


def _layernorm(x, g, b):
    mu = jnp.mean(x, -1, keepdims=True)
    var = jnp.mean(jnp.square(x - mu), -1, keepdims=True)
    return (x - mu) * lax.rsqrt(var + EPS) * g + b


def _gelu(x):
    return jax.nn.gelu(x, approximate=True)


def _sigmoid(x):
    return 1.0 / (1.0 + jnp.exp(-x))


def _rope_block(blk, ct, sa, sb):
    return blk * ct + pltpu.roll(blk, LANE - QK_ROPE // 2, 1) * sa + pltpu.roll(blk, QK_ROPE // 2, 1) * sb


def _mla_queries(zq, qng, wuq, wabs_ref, ct, sa, sb, q_ref):
    qn = _rms(zq, qng)
    aq = _bdot(qn, wuq)
    lane = lax.broadcasted_iota(jnp.int32, (1, LANE), 1)
    for h in range(MLA_HEADS):
        blk = aq[:, h * LANE:(h + 1) * LANE]
        qh = _rope_block(blk, ct, sa, sb) * MLA_SCALE
        q_ref[:, h * QW:h * QW + LANE] = _bdot(qh, wabs_ref[h]).astype(q_ref.dtype)
        q_ref[:, h * QW + LANE:(h + 1) * QW] = jnp.where(lane < QK_ROPE, qh, 0.0).astype(q_ref.dtype)


def _inproj_prompt_kernel(x_ref, gpre_ref, win_ref, lng_ref, lnb_ref, wsp_ref, bsp_ref,
                          qng_ref, wuq_ref, wabs_ref, kvg_ref, ct_ref, sa_ref, sb_ref,
                          a_ref, s_ref, q_ref, c_ref, kr_ref, kc_ref):
    x = x_ref[...]
    tm = x.shape[0]
    z = _bdot(_rms(x, gpre_ref[...]), win_ref[...])
    ua = _gelu(z[:, :GM_WIDTH])
    vn = _layernorm(_gelu(z[:, GM_WIDTH:2 * GM_WIDTH]), lng_ref[...], lnb_ref[...])

    tri = (lax.broadcasted_iota(jnp.int32, (CHUNK, CHUNK), 0)
           >= lax.broadcasted_iota(jnp.int32, (CHUNK, CHUNK), 1))
    lane_head = lax.broadcasted_iota(jnp.int32, (1, GM_WIDTH), 1) // GM_HEAD_DIM
    ws = [jnp.where(tri, wsp_ref[h], 0.0).astype(BF16) for h in range(GM_HEADS)]
    for c in range(tm // CHUNK):
        vc = vn[c * CHUNK:(c + 1) * CHUNK]
        mixed = bsp_ref[...]
        for h in range(GM_HEADS):
            mixed = mixed + _bdot(ws[h], jnp.where(lane_head == h, vc, 0.0))
        a_ref[c * CHUNK:(c + 1) * CHUNK, :] = (ua[c * CHUNK:(c + 1) * CHUNK] * mixed).astype(a_ref.dtype)

    s_ref[...] = z[:, 2 * GM_WIDTH:2 * GM_WIDTH + SSM_WIDTH]

    ct, sa, sb = ct_ref[...], sa_ref[...], sb_ref[...]
    q0 = 2 * GM_WIDTH + SSM_WIDTH
    _mla_queries(z[:, q0:q0 + Q_LORA], qng_ref[...], wuq_ref[...], wabs_ref, ct, sa, sb, q_ref)

    c = _rms(z[:, q0 + Q_LORA:q0 + Q_LORA + KV_LORA], kvg_ref[...])
    krb = _rope_block(z[:, q0 + Q_LORA + KV_LORA:], ct, sa, sb)
    c_ref[...] = c
    kr_ref[...] = krb[:, :QK_ROPE]
    kc_ref[:, :LANE] = c.astype(kc_ref.dtype)
    kc_ref[:, LANE:] = krb.astype(kc_ref.dtype)


def _inproj_sample_kernel(x_ref, gpre_ref, win_ref, lng_ref, lnb_ref, wd_ref, b0_ref,
                          qng_ref, wuq_ref, wabs_ref, kvg_ref, ct_ref, sa_ref, sb_ref,
                          h0r_ref, h0i_ref, abr_ref, abi_ref, wbr_ref, wbi_ref, wcr_ref, wci_ref,
                          dsk_ref, wglu_ref, bglu_ref,
                          a_ref, b_ref, q_ref, c_ref, kr_ref, snew_ref, vn_ref, hr_ref, hi_ref):
    x = x_ref[...]
    z = _bdot(_rms(x, gpre_ref[...]), win_ref[...])
    ua = _gelu(z[:, :GM_WIDTH])
    vn = _layernorm(_gelu(z[:, GM_WIDTH:2 * GM_WIDTH]), lng_ref[...], lnb_ref[...])
    vn_ref[...] = vn
    a_ref[...] = (ua * (wd_ref[...] * vn + b0_ref[...])).astype(a_ref.dtype)

    s = z[:, 2 * GM_WIDTH:2 * GM_WIDTH + SSM_WIDTH]
    h0r, h0i = h0r_ref[...], h0i_ref[...]
    abr, abi = abr_ref[...], abi_ref[...]
    hr = abr * h0r - abi * h0i + _bdot(s, wbr_ref[...])
    hi = abr * h0i + abi * h0r + _bdot(s, wbi_ref[...])
    hr_ref[...] = hr
    hi_ref[...] = hi
    y = _bdot(hr, wcr_ref[...]) - _bdot(hi, wci_ref[...]) + dsk_ref[...] * s
    y = _gelu(y)
    b_ref[...] = (y * _sigmoid(_bdot(y, wglu_ref[...]) + bglu_ref[...])).astype(b_ref.dtype)

    ct, sa, sb = ct_ref[...], sa_ref[...], sb_ref[...]
    q0 = 2 * GM_WIDTH + SSM_WIDTH
    _mla_queries(z[:, q0:q0 + Q_LORA], qng_ref[...], wuq_ref[...], wabs_ref, ct, sa, sb, q_ref)
    c = _rms(z[:, q0 + Q_LORA:q0 + Q_LORA + KV_LORA], kvg_ref[...])
    krb = _rope_block(z[:, q0 + Q_LORA + KV_LORA:], ct, sa, sb)
    c_ref[...] = c
    kr_ref[...] = krb[:, :QK_ROPE]
    cb, kb = c.astype(BF16).astype(F32), krb.astype(BF16).astype(F32)
    cols = []
    for h in range(MLA_HEADS):
        ql = q_ref[:, h * QW:h * QW + LANE].astype(F32)
        qr = q_ref[:, h * QW + LANE:(h + 1) * QW].astype(F32)
        cols.append(jnp.sum(ql * cb + qr * kb, -1, keepdims=True))
    snew_ref[...] = jnp.concatenate(cols, -1)


def _s5_scan_kernel(s_ref, e_ref, et_ref, ab_ref, wb_ref, wc_ref, dsk_ref, wglu_ref, bglu_ref,
                    o_ref, hr_out, hi_out, xr_s, xi_s, hr_s, hi_s):
    i = pl.program_id(0)
    nb, tt, _ = s_ref.shape
    rows = xr_s.shape[0]

    @pl.when(i == 0)
    def _():
        hr_s[...] = jnp.zeros_like(hr_s)
        hi_s[...] = jnp.zeros_like(hi_s)

    rowq = lax.broadcasted_iota(jnp.int32, (rows, SSM_WIDTH), 0) % 4
    laneq = lax.broadcasted_iota(jnp.int32, (rows, SSM_WIDTH), 1) // (SSM_WIDTH // 4)
    own = rowq == laneq

    uexp = _bdot(e_ref[0], s_ref[0])
    for b in range(1, nb):
        uexp = uexp + _bdot(e_ref[b], s_ref[b])
    um = jnp.where(own, uexp, 0.0).astype(BF16)
    xr_s[...] = jnp.dot(um, wb_ref[0], preferred_element_type=F32)
    xi_s[...] = jnp.dot(um, wb_ref[1], preferred_element_type=F32)

    ar, ai = ab_ref[0], ab_ref[1]

    def step(t, carry):
        hr, hi = carry
        r = pl.ds(pl.multiple_of(t * 8, 8), 8)
        nr = ar * hr - ai * hi + xr_s[r, :]
        ni = ar * hi + ai * hr + xi_s[r, :]
        xr_s[r, :] = nr
        xi_s[r, :] = ni
        return nr, ni

    hr, hi = lax.fori_loop(0, tt, step, (hr_s[...], hi_s[...]), unroll=8)
    hr_s[...] = hr
    hi_s[...] = hi
    hr_out[...] = hr
    hi_out[...] = hi

    yp = _bdot(xr_s[...], wc_ref[0]) - _bdot(xi_s[...], wc_ref[1])
    ym = jnp.where(own, yp, 0.0)
    y_hi = ym.astype(BF16)
    y_lo = (ym - y_hi.astype(F32)).astype(BF16)
    for b in range(nb):
        s = s_ref[b]
        y = (jnp.dot(et_ref[b], y_hi, preferred_element_type=F32)
             + jnp.dot(et_ref[b], y_lo, preferred_element_type=F32)) + dsk_ref[...] * s
        y = _gelu(y)
        o_ref[b] = (y * _sigmoid(_bdot(y, wglu_ref[...]) + bglu_ref[...])).astype(o_ref.dtype)


def _flash_kernel(q_ref, kc_ref, o_ref, q2_s, m_s, l_s, acc_s, *, tq, tk):
    qi = pl.program_id(1)
    ki = pl.program_id(2)
    last_k = (qi * tq + tq - 1) // tk

    @pl.when(ki == 0)
    def _():
        for h in range(MLA_HEADS):
            q2_s[h * tq:(h + 1) * tq, :] = q_ref[:, h * QW:(h + 1) * QW]
        m_s[...] = jnp.full_like(m_s, -jnp.inf)
        l_s[...] = jnp.zeros_like(l_s)
        acc_s[...] = jnp.zeros_like(acc_s)

    def step(masked):
        kc = kc_ref[...]
        s = lax.dot_general(q2_s[...], kc, (((1,), (1,)), ((), ())), preferred_element_type=F32)
        if masked:
            qpos = qi * tq + lax.broadcasted_iota(jnp.int32, s.shape, 0) % tq
            kpos = ki * tk + lax.broadcasted_iota(jnp.int32, s.shape, 1)
            s = jnp.where(kpos <= qpos, s, NEG)
        m_prev = m_s[...]
        m_new = jnp.maximum(m_prev, jnp.max(s, -1, keepdims=True))
        alpha = jnp.exp(m_prev - m_new)
        p = jnp.exp(s - m_new)
        l_s[...] = alpha * l_s[...] + jnp.sum(p, -1, keepdims=True)
        acc_s[...] = alpha * acc_s[...] + jnp.dot(p.astype(BF16), kc[:, :LANE], preferred_element_type=F32)
        m_s[...] = m_new

    full = ki * tk + tk - 1 <= qi * tq

    @pl.when(full)
    def _():
        step(False)

    @pl.when(jnp.logical_and(jnp.logical_not(full), ki <= last_k))
    def _():
        step(True)

    @pl.when(ki == last_k)
    def _():
        o = acc_s[...] / l_s[...]
        for h in range(MLA_HEADS):
            o_ref[:, h * LANE:(h + 1) * LANE] = o[h * tq:(h + 1) * tq].astype(o_ref.dtype)


def _paged_kernel(pt_ref, q_ref, snew_ref, cnew_ref, *refs, n_pages):
    c_refs = refs[:n_pages]
    kr_refs = refs[n_pages:2 * n_pages]
    o_ref, m_s, l_s, acc_s = refs[2 * n_pages:]
    j = pl.program_id(1)

    @pl.when(j == 0)
    def _():
        m_s[...] = jnp.full_like(m_s, -jnp.inf)
        l_s[...] = jnp.zeros_like(l_s)
        acc_s[...] = jnp.zeros_like(acc_s)

    q = q_ref[...]
    ql, qr = q[:, :LANE], q[:, LANE:LANE + QK_ROPE]
    cs = [c_refs[i][...].astype(BF16) for i in range(n_pages)]
    s = jnp.concatenate(
        [lax.dot_general(ql, cs[i], (((1,), (1,)), ((), ())), preferred_element_type=F32)
         + lax.dot_general(qr, kr_refs[i][...].astype(BF16), (((1,), (1,)), ((), ())), preferred_element_type=F32)
         for i in range(n_pages)], -1)
    m_prev = m_s[...]
    m_new = jnp.maximum(m_prev, jnp.max(s, -1, keepdims=True))
    alpha = jnp.exp(m_prev - m_new)
    p = jnp.exp(s - m_new)
    l_s[...] = alpha * l_s[...] + jnp.sum(p, -1, keepdims=True)
    pv = jnp.dot(p[:, :PAGE_SIZE].astype(BF16), cs[0], preferred_element_type=F32)
    for i in range(1, n_pages):
        pv = pv + jnp.dot(p[:, i * PAGE_SIZE:(i + 1) * PAGE_SIZE].astype(BF16), cs[i], preferred_element_type=F32)
    acc_s[...] = alpha * acc_s[...] + pv
    m_s[...] = m_new

    @pl.when(j == pl.num_programs(1) - 1)
    def _():
        s_new = snew_ref[...]
        m_f = jnp.maximum(m_s[...], s_new)
        a_f = jnp.exp(m_s[...] - m_f)
        p_new = jnp.exp(s_new - m_f)
        c_new = cnew_ref[...].astype(BF16).astype(F32)
        l_f = a_f * l_s[...] + p_new
        o_ref[...] = ((a_f * acc_s[...] + p_new.astype(BF16).astype(F32) * c_new) / l_f).astype(o_ref.dtype)


def _memkv_kernel(mem_ref, g_ref, wk_ref, wv_ref, k_ref, v_ref):
    mn = _rms(mem_ref[...], g_ref[...]).astype(BF16)
    k_ref[...] = jnp.dot(mn, wk_ref[...], preferred_element_type=F32)
    v_ref[...] = jnp.dot(mn, wv_ref[...], preferred_element_type=F32)


def _mem_heads(q, mk, mv, o_ref):
    d = q.shape[-1] // MEM_HEADS
    scale = d ** -0.5
    for h in range(MEM_HEADS):
        sl = slice(h * d, (h + 1) * d)
        s = _bdot_nt(q[:, sl], mk[:, sl]) * scale
        p = jnp.exp(s - jnp.max(s, -1, keepdims=True))
        p = p / jnp.sum(p, -1, keepdims=True)
        o_ref[:, sl] = _bdot(p, mv[:, sl]).astype(o_ref.dtype)


def _memattn_kernel(q_ref, mk_ref, mv_ref, o_ref):
    _mem_heads(q_ref[...], mk_ref[...].astype(BF16), mv_ref[...].astype(BF16), o_ref)


def _memattn_sample_kernel(q_ref, mk_ref, mv_ref, o_ref, o8_s):
    q = jnp.broadcast_to(q_ref[...], (8, q_ref.shape[-1]))
    _mem_heads(q, mk_ref[...].astype(BF16), mv_ref[...].astype(BF16), o8_s)
    o_ref[...] = o8_s[0:1, :].astype(o_ref.dtype)


def _post1_kernel(x_ref, a_ref, b_ref, ol_ref, wuv_ref, wout_ref, gpm_ref, gmem_ref, wmq_ref,
                  x1_ref, qm_ref):
    c_out = jnp.dot(ol_ref[...], wuv_ref[...], preferred_element_type=F32).astype(BF16)
    mixed = jnp.concatenate([a_ref[...], b_ref[...], c_out], -1)
    mix = jnp.dot(mixed, wout_ref[...], preferred_element_type=F32)
    x1 = x_ref[...] + _rms(mix, gpm_ref[...])
    x1_ref[...] = x1
    qm_ref[...] = _bdot(_rms(x1, gmem_ref[...]), wmq_ref[...]).astype(qm_ref.dtype)


def _post2_kernel(x1_ref, om_ref, wmo_ref, gpm_ref, gffn_ref, wg_ref, wu_ref, wd_ref, gpf_ref, x3_ref):
    m = _bdot(om_ref[...], wmo_ref[...])
    x2 = x1_ref[...] + _rms(m, gpm_ref[...])
    hf = _rms(x2, gffn_ref[...]).astype(BF16)
    gate = jnp.dot(hf, wg_ref[...], preferred_element_type=F32)
    up = jnp.dot(hf, wu_ref[...], preferred_element_type=F32)
    act = (gate * _sigmoid(gate) * up).astype(BF16)
    f = jnp.dot(act, wd_ref[...], preferred_element_type=F32)
    x3_ref[...] = x2 + _rms(f, gpf_ref[...])


def _const_spec(shape):
    n = len(shape)
    return pl.BlockSpec(shape, lambda *_: (0,) * n)


def _params(sem):
    return pltpu.CompilerParams(dimension_semantics=sem, vmem_limit_bytes=VMEM_LIMIT)


def _row_tile(n, pref):
    t = min(n, pref)
    while n % t:
        t //= 2
    return t


def _inproj_prompt(x, lw, tabs):
    bsz, seq, d = x.shape
    tm = _row_tile(seq, 512)
    row = lambda w, dt=None: pl.BlockSpec((None, tm, w), lambda b, i: (b, i, 0))
    tab = pl.BlockSpec((tm, LANE), lambda b, i: (i, 0))
    consts = [lw['g_pre_mix'], lw['w_in_p'], lw['gm_ln_g'], lw['gm_ln_b'], lw['gm_w_sp'], lw['gm_bias_mat'],
              lw['mla_q_norm'], lw['w_uq_p'], lw['w_abs'], lw['mla_kv_norm']]
    out_shape = (jax.ShapeDtypeStruct((bsz, seq, GM_WIDTH), BF16),
                 jax.ShapeDtypeStruct((bsz, seq, SSM_WIDTH), F32),
                 jax.ShapeDtypeStruct((bsz, seq, MLA_HEADS * QW), BF16),
                 jax.ShapeDtypeStruct((bsz, seq, KV_LORA), F32),
                 jax.ShapeDtypeStruct((bsz, seq, QK_ROPE), F32),
                 jax.ShapeDtypeStruct((bsz, seq, QW), BF16))
    return pl.pallas_call(
        _inproj_prompt_kernel, out_shape=out_shape, grid=(bsz, seq // tm),
        in_specs=[row(d)] + [_const_spec(c.shape) for c in consts] + [tab, tab, tab],
        out_specs=(row(GM_WIDTH), row(SSM_WIDTH), row(MLA_HEADS * QW), row(KV_LORA), row(QK_ROPE), row(QW)),
        compiler_params=_params(("parallel", "parallel")), name="inproj_prompt",
    )(x, *consts, *tabs)


def _inproj_sample(x, h0r, h0i, lw, tabs):
    n, d = x.shape
    args = [x, lw['g_pre_mix'], lw['w_in_p'], lw['gm_ln_g'], lw['gm_ln_b'], lw['gm_w0'], lw['gm_b0'],
            lw['mla_q_norm'], lw['w_uq_p'], lw['w_abs'], lw['mla_kv_norm'], *tabs,
            h0r, h0i, lw['ab_re'], lw['ab_im'], lw['wb_re'], lw['wb_im'], lw['wc_re'], lw['wc_im'],
            lw['ssm_d'], lw['ssm_w_glu'], lw['ssm_b_glu']]
    out_shape = (jax.ShapeDtypeStruct((n, GM_WIDTH), BF16),
                 jax.ShapeDtypeStruct((n, SSM_WIDTH), BF16),
                 jax.ShapeDtypeStruct((n, MLA_HEADS * QW), BF16),
                 jax.ShapeDtypeStruct((n, KV_LORA), F32),
                 jax.ShapeDtypeStruct((n, QK_ROPE), F32),
                 jax.ShapeDtypeStruct((n, MLA_HEADS), F32),
                 jax.ShapeDtypeStruct((n, GM_WIDTH), F32),
                 jax.ShapeDtypeStruct((n, SSM_STATES), F32),
                 jax.ShapeDtypeStruct((n, SSM_STATES), F32))
    return pl.pallas_call(
        _inproj_sample_kernel, out_shape=out_shape, grid=(1,),
        in_specs=[_const_spec(a.shape) for a in args],
        out_specs=tuple(_const_spec(o.shape) for o in out_shape),
        compiler_params=_params(("arbitrary",)), name="inproj_sample",
    )(*args)


def _s5_scan(s_b, lw):
    bsz, seq, w = s_b.shape
    tt = _row_tile(seq, 256)
    rows = tt * bsz * 4
    assert rows == tt * 8, "state tile holds (batch, quarter) rows on the 8 sublanes"
    t_idx = jnp.arange(rows) // 8
    b_idx = (jnp.arange(rows) // 4) % bsz
    e = ((t_idx[None, :, None] == jnp.arange(tt)[None, None, :])
         & (b_idx[None, :, None] == jnp.arange(bsz)[:, None, None])).astype(BF16)
    et = jnp.swapaxes(e, 1, 2)
    consts = [e, et, lw['ab_tile'], lw['wb_c'], lw['wc_c'], lw['ssm_d'], lw['ssm_w_glu'], lw['ssm_b_glu']]
    blk = pl.BlockSpec((bsz, tt, w), lambda i: (0, i, 0))
    st = jax.ShapeDtypeStruct((8, SSM_WIDTH), F32)
    return pl.pallas_call(
        _s5_scan_kernel,
        out_shape=(jax.ShapeDtypeStruct((bsz, seq, w), BF16), st, st),
        grid=(seq // tt,),
        in_specs=[blk] + [_const_spec(c.shape) for c in consts],
        out_specs=(blk, _const_spec(st.shape), _const_spec(st.shape)),
        scratch_shapes=[pltpu.VMEM((rows, SSM_WIDTH), F32), pltpu.VMEM((rows, SSM_WIDTH), F32),
                        pltpu.VMEM((8, SSM_WIDTH), F32), pltpu.VMEM((8, SSM_WIDTH), F32)],
        compiler_params=_params(("arbitrary",)), name="s5_scan",
    )(s_b, *consts)


def _flash(q_abs, kc):
    bsz, seq, _ = q_abs.shape
    tq = _row_tile(seq, 256)
    tk = _row_tile(seq, 512)
    kern = functools.partial(_flash_kernel, tq=tq, tk=tk)
    return pl.pallas_call(
        kern, out_shape=jax.ShapeDtypeStruct((bsz, seq, MLA_HEADS * LANE), BF16),
        grid=(bsz, seq // tq, seq // tk),
        in_specs=[pl.BlockSpec((None, tq, MLA_HEADS * QW), lambda b, i, k: (b, i, 0)),
                  pl.BlockSpec((None, tk, QW), lambda b, i, k: (b, jnp.minimum(k, (i * tq + tq - 1) // tk), 0))],
        out_specs=pl.BlockSpec((None, tq, MLA_HEADS * LANE), lambda b, i, k: (b, i, 0)),
        scratch_shapes=[pltpu.VMEM((MLA_HEADS * tq, QW), BF16), pltpu.VMEM((MLA_HEADS * tq, 1), F32),
                        pltpu.VMEM((MLA_HEADS * tq, 1), F32), pltpu.VMEM((MLA_HEADS * tq, LANE), F32)],
        compiler_params=_params(("parallel", "parallel", "arbitrary")), name="flash_prompt",
    )(q_abs, kc)


def _paged_attn(layer, q_abs, s_new, c_new, cache_c, cache_kr, page_table):
    n, n_pages_total = page_table.shape
    per_step = math.gcd(n_pages_total, 16)
    q3 = q_abs.reshape(n, MLA_HEADS, QW)
    c_spec = lambda i: pl.BlockSpec((None, None, PAGE_SIZE, KV_LORA),
                                    lambda b, j, pt: (layer, pt[b, j * per_step + i], 0, 0))
    kr_spec = lambda i: pl.BlockSpec((None, None, PAGE_SIZE, QK_ROPE),
                                     lambda b, j, pt: (layer, pt[b, j * per_step + i], 0, 0))
    grid_spec = pltpu.PrefetchScalarGridSpec(
        num_scalar_prefetch=1, grid=(n, n_pages_total // per_step),
        in_specs=[pl.BlockSpec((None, MLA_HEADS, QW), lambda b, j, pt: (b, 0, 0)),
                  pl.BlockSpec((None, MLA_HEADS, 1), lambda b, j, pt: (b, 0, 0)),
                  pl.BlockSpec((None, 1, KV_LORA), lambda b, j, pt: (b, 0, 0))]
                 + [c_spec(i) for i in range(per_step)] + [kr_spec(i) for i in range(per_step)],
        out_specs=pl.BlockSpec((None, MLA_HEADS, KV_LORA), lambda b, j, pt: (b, 0, 0)),
        scratch_shapes=[pltpu.VMEM((MLA_HEADS, 1), F32), pltpu.VMEM((MLA_HEADS, 1), F32),
                        pltpu.VMEM((MLA_HEADS, KV_LORA), F32)])
    out = pl.pallas_call(
        functools.partial(_paged_kernel, n_pages=per_step),
        out_shape=jax.ShapeDtypeStruct((n, MLA_HEADS, KV_LORA), BF16), grid_spec=grid_spec,
        compiler_params=_params(("parallel", "arbitrary")), name="paged_attn",
    )(page_table, q3, s_new.reshape(n, MLA_HEADS, 1), c_new.reshape(n, 1, KV_LORA),
      *([cache_c] * per_step), *([cache_kr] * per_step))
    return out.reshape(n, MLA_HEADS * KV_LORA)


def _mem_kv(mem, g, wk, wv):
    bsz, m, d = mem.shape
    blk = pl.BlockSpec((None, m, d), lambda b: (b, 0, 0))
    o = jax.ShapeDtypeStruct((bsz, m, d), F32)
    return pl.pallas_call(
        _memkv_kernel, out_shape=(o, o), grid=(bsz,),
        in_specs=[blk, _const_spec(g.shape), _const_spec(wk.shape), _const_spec(wv.shape)],
        out_specs=(blk, blk), compiler_params=_params(("parallel",)), name="mem_kv",
    )(mem, g, wk, wv)


def _memattn_prompt(qm, mk, mv):
    bsz, seq, d = qm.shape
    tm = _row_tile(seq, 512)
    row = pl.BlockSpec((None, tm, d), lambda b, i: (b, i, 0))
    mem = pl.BlockSpec((None,) + mk.shape[1:], lambda b, i: (b, 0, 0))
    return pl.pallas_call(
        _memattn_kernel, out_shape=jax.ShapeDtypeStruct((bsz, seq, d), BF16), grid=(bsz, seq // tm),
        in_specs=[row, mem, mem], out_specs=row,
        compiler_params=_params(("parallel", "parallel")), name="memattn_prompt",
    )(qm, mk, mv)


def _memattn_sample(layer, qm, cache_k, cache_v):
    n, d = qm.shape
    m = cache_k.shape[2]
    row = pl.BlockSpec((None, 1, d), lambda b: (b, 0, 0))
    mem = pl.BlockSpec((None, None, m, d), lambda b: (layer, b, 0, 0))
    out = pl.pallas_call(
        _memattn_sample_kernel, out_shape=jax.ShapeDtypeStruct((n, 1, d), F32), grid=(n,),
        in_specs=[row, mem, mem], out_specs=row,
        scratch_shapes=[pltpu.VMEM((8, d), F32)],
        compiler_params=_params(("parallel",)), name="memattn_sample",
    )(qm.reshape(n, 1, d), cache_k, cache_v)
    return out.reshape(n, d)


def _post1(x, a, b, ol, lw, qm_dtype):
    n, d = x.shape
    tm = _row_tile(n, 512)
    row = lambda w: pl.BlockSpec((tm, w), lambda i: (i, 0))
    consts = [lw['w_uv_bd'], lw['w_out'], lw['g_post_mix'], lw['g_pre_mem'], lw['mem_w_q']]
    return pl.pallas_call(
        _post1_kernel,
        out_shape=(jax.ShapeDtypeStruct((n, d), F32), jax.ShapeDtypeStruct((n, d), qm_dtype)),
        grid=(n // tm,),
        in_specs=[row(d), row(a.shape[1]), row(b.shape[1]), row(ol.shape[1])] + [_const_spec(c.shape) for c in consts],
        out_specs=(row(d), row(d)),
        compiler_params=_params(("parallel",)), name="post1",
    )(x, a, b, ol, *consts)


def _post2(x1, om, lw):
    n, d = x1.shape
    tm = _row_tile(n, 256)
    row = pl.BlockSpec((tm, d), lambda i: (i, 0))
    consts = [lw['mem_w_o'], lw['g_post_mem'], lw['g_pre_ffn'], lw['ffn_w_gate'], lw['ffn_w_up'],
              lw['ffn_w_down'], lw['g_post_ffn']]
    single = lambda c: pl.BlockSpec(c.shape, lambda i: (0,) * c.ndim, pipeline_mode=pl.Buffered(1))
    return pl.pallas_call(
        _post2_kernel, out_shape=jax.ShapeDtypeStruct((n, d), F32), grid=(n // tm,),
        in_specs=[row, row] + [single(c) for c in consts], out_specs=row,
        compiler_params=_params(("parallel",)), name="post2",
    )(x1, om, *consts)


def _rope_tables(pos):
    half = QK_ROPE // 2
    inv = ROPE_THETA ** (-jnp.arange(half, dtype=F32) / half)
    ang = pos.astype(F32)[:, None] * inv[None, :]
    cos, sin = jnp.cos(ang), jnp.sin(ang)
    n = pos.shape[0]
    ones = jnp.ones((n, LANE - QK_ROPE), F32)
    zeros = jnp.zeros((n, LANE - half), F32)
    ct = jnp.concatenate([cos, cos, ones], -1)
    sa = jnp.concatenate([-sin, zeros], -1)
    sb = jnp.concatenate([jnp.zeros((n, half), F32), sin, jnp.zeros((n, LANE - QK_ROPE), F32)], -1)
    return ct, sa, sb


def _ssm_discretise(a_re, a_im, log_dt, b_re, b_im):
    dt = jnp.exp(log_dt)[:, None]
    mag = jnp.exp(a_re * dt)
    ab_re = mag * jnp.cos(a_im * dt)
    ab_im = mag * jnp.sin(a_im * dt)
    er, ei = ab_re - 1.0, ab_im
    den = a_re * a_re + a_im * a_im
    fr = ((er * a_re + ei * a_im) / den)[..., None]
    fi = ((ei * a_re - er * a_im) / den)[..., None]
    return ab_re, ab_im, fr * b_re - fi * b_im, fr * b_im + fi * b_re


def _block_diag(blocks):
    g, r, c = blocks.shape
    eye = jnp.eye(g, dtype=blocks.dtype)
    return (eye[:, None, :, None] * blocks[:, :, None, :]).reshape(g * r, g * c)


def _layer_weights(l, p):
    lw = {}
    vec = lambda name: p[name][l][None, :]
    for name in ('g_pre_mix', 'g_post_mix', 'g_pre_mem', 'g_post_mem', 'g_pre_ffn', 'g_post_ffn',
                 'gm_ln_g', 'gm_ln_b', 'mla_q_norm', 'mla_kv_norm', 'ssm_d', 'ssm_b_glu', 'mem_norm'):
        lw[name] = vec(name)
    for name in ('w_out', 'ssm_w_glu', 'mem_w_q', 'mem_w_k', 'mem_w_v', 'mem_w_o',
                 'ffn_w_gate', 'ffn_w_up', 'ffn_w_down'):
        lw[name] = p[name][l].astype(BF16)
    w_in = p['w_in'][l]
    lw['w_in_p'] = jnp.pad(w_in, ((0, 0), (0, IN_PAD - w_in.shape[1]))).astype(BF16)
    lw['gm_w_sp'] = p['gm_w_sp'][l]
    b_sp = p['gm_b_sp'][l]
    lw['gm_bias_mat'] = jnp.repeat(b_sp.T, GM_HEAD_DIM, axis=1)
    lw['gm_w0'] = jnp.repeat(p['gm_w_sp'][l][:, 0, 0], GM_HEAD_DIM)[None, :]
    lw['gm_b0'] = jnp.repeat(b_sp[:, 0], GM_HEAD_DIM)[None, :]
    ab_re, ab_im, bb_re, bb_im = _ssm_discretise(p['ssm_a_re'][l], p['ssm_a_im'][l], p['ssm_log_dt'][l],
                                                 p['ssm_b_re'][l], p['ssm_b_im'][l])
    lw['ab_re'] = ab_re.reshape(1, SSM_STATES)
    lw['ab_im'] = ab_im.reshape(1, SSM_STATES)
    quarter = lambda v: jnp.tile(v.reshape(4, SSM_STATES // 4), (2, 1))
    lw['ab_tile'] = jnp.stack([quarter(ab_re), quarter(ab_im)])
    wb_re = _block_diag(jnp.swapaxes(bb_re, 1, 2))
    wb_im = _block_diag(jnp.swapaxes(bb_im, 1, 2))
    wc_re = _block_diag(jnp.swapaxes(p['ssm_c_re'][l], 1, 2))
    wc_im = _block_diag(jnp.swapaxes(p['ssm_c_im'][l], 1, 2))
    lw['wb_re'], lw['wb_im'] = wb_re.astype(BF16), wb_im.astype(BF16)
    lw['wc_re'], lw['wc_im'] = wc_re.astype(BF16), wc_im.astype(BF16)
    compact_b = lambda w: w.reshape(SSM_WIDTH, 4, SSM_STATES // 4).sum(1)
    compact_c = lambda w: w.reshape(4, SSM_STATES // 4, SSM_WIDTH).sum(0)
    lw['wb_c'] = jnp.stack([compact_b(wb_re), compact_b(wb_im)]).astype(BF16)
    lw['wc_c'] = jnp.stack([compact_c(wc_re), compact_c(wc_im)]).astype(BF16)
    w_uq = p['mla_w_uq'][l].reshape(Q_LORA, MLA_HEADS, QK_NOPE + QK_ROPE)
    w_uq_p = jnp.concatenate([w_uq[..., QK_NOPE:], w_uq[..., :QK_NOPE],
                              jnp.zeros((Q_LORA, MLA_HEADS, LANE - QK_NOPE - QK_ROPE), F32)], -1)
    lw['w_uq_p'] = w_uq_p.reshape(Q_LORA, MLA_HEADS * LANE).astype(BF16)
    w_uk_t = jnp.transpose(p['mla_w_uk'][l], (1, 2, 0))
    lw['w_abs'] = jnp.pad(w_uk_t, ((0, 0), (QK_ROPE, LANE - QK_NOPE - QK_ROPE), (0, 0))).astype(BF16)
    lw['w_uv_bd'] = _block_diag(jnp.transpose(p['mla_w_uv'][l], (1, 0, 2))).astype(BF16)
    return lw


def kernel(x_prompt, x_sample, cache_kv_latent, cache_k_rope, cache_mem_k, cache_mem_v, state_ssm_re, state_ssm_im, page_table, mem_prompt, g_pre_mix, g_post_mix, g_pre_mem, g_post_mem, g_pre_ffn, g_post_ffn, w_in, w_out, gm_ln_g, gm_ln_b, gm_w_sp, gm_b_sp, ssm_a_re, ssm_a_im, ssm_log_dt, ssm_b_re, ssm_b_im, ssm_c_re, ssm_c_im, ssm_d, ssm_w_glu, ssm_b_glu, mla_q_norm, mla_kv_norm, mla_w_uq, mla_w_uk, mla_w_uv, mem_norm, mem_w_q, mem_w_k, mem_w_v, mem_w_o, ffn_w_gate, ffn_w_up, ffn_w_down):
    params = dict(g_pre_mix=g_pre_mix, g_post_mix=g_post_mix, g_pre_mem=g_pre_mem, g_post_mem=g_post_mem,
                  g_pre_ffn=g_pre_ffn, g_post_ffn=g_post_ffn, w_in=w_in, w_out=w_out, gm_ln_g=gm_ln_g,
                  gm_ln_b=gm_ln_b, gm_w_sp=gm_w_sp, gm_b_sp=gm_b_sp, ssm_a_re=ssm_a_re, ssm_a_im=ssm_a_im,
                  ssm_log_dt=ssm_log_dt, ssm_b_re=ssm_b_re, ssm_b_im=ssm_b_im, ssm_c_re=ssm_c_re,
                  ssm_c_im=ssm_c_im, ssm_d=ssm_d, ssm_w_glu=ssm_w_glu, ssm_b_glu=ssm_b_glu,
                  mla_q_norm=mla_q_norm, mla_kv_norm=mla_kv_norm, mla_w_uq=mla_w_uq, mla_w_uk=mla_w_uk,
                  mla_w_uv=mla_w_uv, mem_norm=mem_norm, mem_w_q=mem_w_q, mem_w_k=mem_w_k, mem_w_v=mem_w_v,
                  mem_w_o=mem_w_o, ffn_w_gate=ffn_w_gate, ffn_w_up=ffn_w_up, ffn_w_down=ffn_w_down)
    depth = w_in.shape[0]
    bsz, seq, d = x_prompt.shape
    n_s, dseq, _ = x_sample.shape
    assert dseq == 1, "the sample group decodes one token per sequence"
    n_mem = mem_prompt.shape[1]
    past_len = page_table.shape[1] * PAGE_SIZE

    tabs_p = _rope_tables(jnp.arange(seq))
    tabs_s = _rope_tables(jnp.full((1,), past_len))
    mem_k_cache = cache_mem_k.reshape(cache_mem_k.shape[:3] + (d,))
    mem_v_cache = cache_mem_v.reshape(cache_mem_v.shape[:3] + (d,))

    xp = x_prompt
    xs = x_sample.reshape(n_s, d)
    outs = {k: [] for k in ('kvl_p', 'kr_p', 'mk_p', 'mv_p', 'sre_p', 'sim_p',
                            'kvl_s', 'kr_s', 'sre_s', 'sim_s', 'gv_s')}
    for l in range(depth):
        lw = _layer_weights(l, params)

        mk, mv = _mem_kv(mem_prompt, lw['mem_norm'], lw['mem_w_k'], lw['mem_w_v'])
        a_out, s_b, q_abs, c, kr, kc = _inproj_prompt(xp, lw, tabs_p)
        b_out, hre, him = _s5_scan(s_b, lw)
        o_lat = _flash(q_abs, kc)
        x1, qm = _post1(xp.reshape(bsz * seq, d), a_out.reshape(bsz * seq, -1), b_out.reshape(bsz * seq, -1),
                        o_lat.reshape(bsz * seq, -1), lw, BF16)
        om = _memattn_prompt(qm.reshape(bsz, seq, d), mk, mv)
        xp = _post2(x1, om.reshape(bsz * seq, d), lw).reshape(bsz, seq, d)
        outs['kvl_p'].append(c)
        outs['kr_p'].append(kr)
        outs['mk_p'].append(mk.reshape(bsz, n_mem, MEM_HEADS, d // MEM_HEADS))
        outs['mv_p'].append(mv.reshape(bsz, n_mem, MEM_HEADS, d // MEM_HEADS))
        outs['sre_p'].append(hre.reshape(bsz, SSM_GROUPS, SSM_STATE))
        outs['sim_p'].append(him.reshape(bsz, SSM_GROUPS, SSM_STATE))

        (a_s, b_s, q_s, c_s, kr_s, snew, vn_s, hr_s, hi_s) = _inproj_sample(
            xs, state_ssm_re[l].reshape(n_s, SSM_STATES), state_ssm_im[l].reshape(n_s, SSM_STATES), lw, tabs_s)
        ol_s = _paged_attn(l, q_s, snew, c_s, cache_kv_latent, cache_k_rope, page_table)
        x1s, qms = _post1(xs, a_s, b_s, ol_s, lw, F32)
        oms = _memattn_sample(l, qms, mem_k_cache, mem_v_cache)
        xs = _post2(x1s, oms, lw)
        outs['kvl_s'].append(c_s.reshape(n_s, 1, KV_LORA))
        outs['kr_s'].append(kr_s.reshape(n_s, 1, QK_ROPE))
        outs['sre_s'].append(hr_s.reshape(n_s, SSM_GROUPS, SSM_STATE))
        outs['sim_s'].append(hi_s.reshape(n_s, SSM_GROUPS, SSM_STATE))
        outs['gv_s'].append(vn_s.reshape(n_s, 1, GM_WIDTH))

    st = lambda k: jnp.stack(outs[k])
    return (xp, xs.reshape(n_s, 1, d),
            st('kvl_p'), st('kr_p'), st('mk_p'), st('mv_p'), st('sre_p'), st('sim_p'),
            st('kvl_s'), st('kr_s'), st('sre_s'), st('sim_s'), st('gv_s'))
```
